```python
import jax, jax.numpy as jnp
from jax import lax
import numpy as np

D_MODEL = 1024
BATCH = 2
SEQ = 8192
DEPTH = 4

GRID_W = 64
CTX_LEN = 256
N_Q_HEADS = 8
N_KV_HEADS = 2
HEAD_DIM = 64
AXIS_DIM = HEAD_DIM // 2
ROPE_THETA = 10000.0
Q_BLOCK = 128
CONV_DIM = 512
CONV_WIDTH = 31
N_FOURIER_GROUPS = 4
FOURIER_GROUP = D_MODEL // N_FOURIER_GROUPS
N_EXPERTS = 16
EXPERT_FF = 2816
CAPACITY_FACTOR = 2
ATTN_DIM = N_Q_HEADS * HEAD_DIM
KV_DIM = N_KV_HEADS * HEAD_DIM
IN_DIM = ATTN_DIM + 2 * KV_DIM + 2 * CONV_DIM
MIX_DIM = ATTN_DIM + CONV_DIM
N_EVEN = (DEPTH + 1) // 2
N_ODD = DEPTH // 2
DEEPNORM_ALPHA = (2 * DEPTH) ** 0.25
DEEPNORM_BETA = (8 * DEPTH) ** -0.25
NORM_EPS = 1e-6

kernel_name = 'hybrid_attn_conformer_fnet_ecmoe_diffusion'


def layer_norm(x, g, b):
    xf = x.astype(jnp.float32)
    mu = jnp.mean(xf, axis=-1, keepdims=True)
    var = jnp.mean(jnp.square(xf - mu), axis=-1, keepdims=True)
    return ((xf - mu) * lax.rsqrt(var + NORM_EPS) * g.astype(jnp.float32) + b.astype(jnp.float32)).astype(x.dtype)


def rms_norm(x, g):
    xf = x.astype(jnp.float32)
    return (xf * lax.rsqrt(jnp.mean(xf * xf, axis=-1, keepdims=True) + NORM_EPS) * g.astype(jnp.float32)).astype(x.dtype)


def axial_rope_tables(n_tok):
    rows = n_tok // GRID_W
    row = jnp.repeat(jnp.arange(rows, dtype=jnp.float32), GRID_W)
    col = jnp.tile(jnp.arange(GRID_W, dtype=jnp.float32), rows)
    inv_freq = ROPE_THETA ** (-jnp.arange(0, AXIS_DIM, 2, dtype=jnp.float32) / AXIS_DIM)
    ang = jnp.concatenate([row[:, None] * inv_freq, col[:, None] * inv_freq], axis=-1)
    return jnp.cos(ang), jnp.sin(ang)


def apply_rope(x, cos, sin):
    xf = x.astype(jnp.float32).reshape(x.shape[:-1] + (HEAD_DIM // 2, 2))
    x0, x1 = xf[..., 0], xf[..., 1]
    c = cos[None, :, None, :]
    s = sin[None, :, None, :]
    out = jnp.stack([x0 * c - x1 * s, x0 * s + x1 * c], axis=-1)
    return out.reshape(x.shape).astype(x.dtype)


def attend(q, k, v):
    s = jnp.einsum('bqkgd,bskd->bkgqs', q, k).astype(jnp.float32) * (HEAD_DIM ** -0.5)
    p = jax.nn.softmax(s, axis=-1).astype(v.dtype)
    return jnp.einsum('bkgqs,bskd->bqkgd', p, v)


def gqa_attention(q_lat, k_lat, v_lat, q_ctx, k_ctx, v_ctx, need_ctx):
    b, n = q_lat.shape[:2]
    grp = N_Q_HEADS // N_KV_HEADS
    k_all = jnp.concatenate([k_ctx, k_lat], axis=1)
    v_all = jnp.concatenate([v_ctx, v_lat], axis=1)
    n_blk = n // Q_BLOCK
    qb = q_lat.reshape(b, n_blk, Q_BLOCK, N_KV_HEADS, grp, HEAD_DIM).transpose(1, 0, 2, 3, 4, 5)
    o = lax.map(lambda qq: attend(qq, k_all, v_all), qb)
    y_lat = o.transpose(1, 0, 2, 3, 4, 5).reshape(b, n, ATTN_DIM)
    y_ctx = None
    if need_ctx:
        nc = q_ctx.shape[1]
        qc = q_ctx.reshape(b, nc, N_KV_HEADS, grp, HEAD_DIM)
        y_ctx = attend(qc, k_ctx, v_ctx).reshape(b, nc, ATTN_DIM)
    return y_lat, y_ctx


def conformer_conv(a, g, w, bias, ln_g, ln_b):
    u = a * jax.nn.sigmoid(g)
    pad = CONV_WIDTH // 2
    u = lax.conv_general_dilated(u, w[:, None, :].astype(u.dtype), window_strides=(1,), padding=[(pad, pad)],
                                 dimension_numbers=('NWC', 'WIO', 'NWC'), feature_group_count=CONV_DIM) + bias
    return jax.nn.silu(layer_norm(u, ln_g, ln_b))


def mixer_attn_conv(h_lat, h_ctx, cos, sin, w_in, b_in, qn_g, kn_g, conv_w, conv_b, cln_g, cln_b, w_out, b_out, need_ctx):
    splits = [ATTN_DIM, ATTN_DIM + KV_DIM, ATTN_DIM + 2 * KV_DIM, ATTN_DIM + 2 * KV_DIM + CONV_DIM]

    def project(h):
        b, n = h.shape[:2]
        p = h @ w_in + b_in
        q, k, v, a, g = jnp.split(p, splits, axis=-1)
        q = rms_norm(q.reshape(b, n, N_Q_HEADS, HEAD_DIM), qn_g)
        k = rms_norm(k.reshape(b, n, N_KV_HEADS, HEAD_DIM), kn_g)
        v = v.reshape(b, n, N_KV_HEADS, HEAD_DIM)
        return q, k, v, a, g

    ql, kl, vl, al, gl = project(h_lat)
    ql = apply_rope(ql, cos, sin)
    kl = apply_rope(kl, cos, sin)
    qc, kc, vc, ac, gc = project(h_ctx)
    att_l, att_c = gqa_attention(ql, kl, vl, qc, kc, vc, need_ctx)
    conv_l = conformer_conv(al, gl, conv_w, conv_b, cln_g, cln_b)
    y_lat = jnp.concatenate([att_l, conv_l], axis=-1) @ w_out + b_out
    y_ctx = None
    if need_ctx:
        conv_c = conformer_conv(ac, gc, conv_w, conv_b, cln_g, cln_b)
        y_ctx = jnp.concatenate([att_c, conv_c], axis=-1) @ w_out + b_out
    return y_lat, y_ctx


def fourier_mix(h):
    b, n, _ = h.shape
    hg = h.astype(jnp.float32).reshape(b, n, N_FOURIER_GROUPS, FOURIER_GROUP)
    f = jnp.fft.fft2(hg, axes=(1, 3), norm='ortho').real
    return f.reshape(b, n, D_MODEL).astype(h.dtype)


def expert_choice_moe(h, w_r, w_gate, w_up, w_down):
    b, n, _ = h.shape
    cap = CAPACITY_FACTOR * n // N_EXPERTS
    aff = jax.nn.softmax((h @ w_r).astype(jnp.float32), axis=-1)
    gates, idx = lax.top_k(jnp.swapaxes(aff, 1, 2), cap)
    bidx = jnp.arange(b)[:, None, None]
    xg = h[bidx, idx]
    hid = jax.nn.silu(jnp.einsum('becd,edf->becf', xg, w_gate)) * jnp.einsum('becd,edf->becf', xg, w_up)
    ye = jnp.einsum('becf,efd->becd', hid, w_down) * gates[..., None].astype(h.dtype)
    return jnp.zeros_like(h).at[bidx, idx].add(ye)


def setup_inputs(seed: int = 0) -> dict:
    key = jax.random.key(seed)
    ks = jax.random.split(key, 24)
    nrm = jax.random.normal
    f32 = jnp.float32
    d = D_MODEL
    return {
        'x': nrm(ks[0], (BATCH, SEQ, d), f32),
        'c': nrm(ks[1], (BATCH, d), f32),
        'ctx': nrm(ks[2], (BATCH, CTX_LEN, d), f32),
        'c_ctx': nrm(ks[3], (d,), f32),
        'ada_w': nrm(ks[4], (DEPTH, d, 6 * d), f32) * (0.5 * d ** -0.5),
        'ada_b': nrm(ks[5], (DEPTH, 6 * d), f32) * 0.02,
        'ln_g': 1.0 + 0.02 * nrm(ks[6], (DEPTH, 2, d), f32),
        'ln_b': 0.02 * nrm(ks[7], (DEPTH, 2, d), f32),
        'attn_in_w': nrm(ks[8], (N_EVEN, d, IN_DIM), f32) * d ** -0.5,
        'attn_in_b': 0.02 * nrm(ks[9], (N_EVEN, IN_DIM), f32),
        'q_norm_g': 1.0 + 0.02 * nrm(ks[10], (N_EVEN, HEAD_DIM), f32),
        'k_norm_g': 1.0 + 0.02 * nrm(ks[11], (N_EVEN, HEAD_DIM), f32),
        'conv_w': nrm(ks[12], (N_EVEN, CONV_WIDTH, CONV_DIM), f32) * CONV_WIDTH ** -0.5,
        'conv_b': 0.02 * nrm(ks[13], (N_EVEN, CONV_DIM), f32),
        'conv_ln_g': 1.0 + 0.02 * nrm(ks[14], (N_EVEN, CONV_DIM), f32),
        'conv_ln_b': 0.02 * nrm(ks[15], (N_EVEN, CONV_DIM), f32),
        'attn_out_w': nrm(ks[16], (N_EVEN, MIX_DIM, d), f32) * (MIX_DIM ** -0.5 * DEEPNORM_BETA),
        'attn_out_b': 0.02 * nrm(ks[17], (N_EVEN, d), f32),
        'fourier_out_w': nrm(ks[18], (N_ODD, d, d), f32) * (d ** -0.5 * DEEPNORM_BETA),
        'fourier_out_b': 0.02 * nrm(ks[19], (N_ODD, d), f32),
        'router_w': nrm(ks[20], (DEPTH, d, N_EXPERTS), f32) * d ** -0.5,
        'expert_w_gate': nrm(ks[21], (DEPTH, N_EXPERTS, d, EXPERT_FF), f32) * d ** -0.5,
        'expert_w_up': nrm(ks[22], (DEPTH, N_EXPERTS, d, EXPERT_FF), f32) * d ** -0.5,
        'expert_w_down': nrm(ks[23], (DEPTH, N_EXPERTS, EXPERT_FF, d), f32) * (EXPERT_FF ** -0.5 * DEEPNORM_BETA),
    }


def reference(x, c, ctx, c_ctx, ada_w, ada_b, ln_g, ln_b, attn_in_w, attn_in_b, q_norm_g, k_norm_g,
              conv_w, conv_b, conv_ln_g, conv_ln_b, attn_out_w, attn_out_b, fourier_out_w, fourier_out_b,
              router_w, expert_w_gate, expert_w_up, expert_w_down):
    n_tok = x.shape[1]
    cos, sin = axial_rope_tables(n_tok)
    mod_lat = jnp.einsum('bd,lde->lbe', jax.nn.silu(c), ada_w) + ada_b[:, None, :]
    mod_ctx = jnp.einsum('d,lde->le', jax.nn.silu(c_ctx), ada_w) + ada_b
    xl, xc = x, ctx
    for i in range(DEPTH):
        need_ctx = i < DEPTH - 1
        ml = jnp.split(mod_lat[i][:, None, :], 6, axis=-1)
        mc = jnp.split(mod_ctx[i], 6, axis=-1)
        hl = xl * (1 + ml[1]) + ml[0]
        hc = xc * (1 + mc[1]) + mc[0]
        j = i // 2
        if i % 2 == 0:
            yl, yc = mixer_attn_conv(hl, hc, cos, sin, attn_in_w[j], attn_in_b[j], q_norm_g[j], k_norm_g[j],
                                     conv_w[j], conv_b[j], conv_ln_g[j], conv_ln_b[j],
                                     attn_out_w[j], attn_out_b[j], need_ctx)
        else:
            yl = fourier_mix(hl) @ fourier_out_w[j] + fourier_out_b[j]
            yc = (fourier_mix(hc) @ fourier_out_w[j] + fourier_out_b[j]) if need_ctx else None
        xl = layer_norm(DEEPNORM_ALPHA * xl + ml[2] * yl, ln_g[i, 0], ln_b[i, 0])
        hl = xl * (1 + ml[4]) + ml[3]
        yl = expert_choice_moe(hl, router_w[i], expert_w_gate[i], expert_w_up[i], expert_w_down[i])
        xl = layer_norm(DEEPNORM_ALPHA * xl + ml[5] * yl, ln_g[i, 1], ln_b[i, 1])
        if need_ctx:
            xc = layer_norm(DEEPNORM_ALPHA * xc + mc[2] * yc, ln_g[i, 0], ln_b[i, 0])
            hc = xc * (1 + mc[4]) + mc[3]
            yc = expert_choice_moe(hc, router_w[i], expert_w_gate[i], expert_w_up[i], expert_w_down[i])
            xc = layer_norm(DEEPNORM_ALPHA * xc + mc[5] * yc, ln_g[i, 1], ln_b[i, 1])
    return xl
```

```python
import functools
import math

import numpy as np
import jax
import jax.numpy as jnp
from jax import lax
from jax.experimental import pallas as pl
from jax.experimental.pallas import tpu as pltpu

GRID_W = 64
N_Q_HEADS = 8
N_KV_HEADS = 2
HEAD_DIM = 64
ROPE_THETA = 10000.0
CONV_WIDTH = 31
N_FOURIER_GROUPS = 4
CAPACITY_FACTOR = 2
NORM_EPS = 1e-6

LANES = 128
SUBLANES = 8
VMEM_BYTES_V7X = 64 * 1024 * 1024
VMEM_LIMIT = VMEM_BYTES_V7X - 8 * 1024 * 1024

ROW_TILE = 256
CONV_HALO = 16
DFT_INNER = 128
LOG2_E = math.log2(math.e)
MAX_UNSHIFTED_LOG2_SCORE = 96.0
EXP_BISECT_STEPS = 9
VALUE_BISECT_STEPS = 32

F32 = jnp.float32
BF16 = jnp.bfloat16


def _params(*sem):
    return pltpu.CompilerParams(dimension_semantics=sem, vmem_limit_bytes=VMEM_LIMIT)


def _layer_norm(v, g, b):
    mu = jnp.mean(v, axis=-1, keepdims=True)
    d = v - mu
    var = jnp.mean(d * d, axis=-1, keepdims=True)
    return d * lax.rsqrt(var + NORM_EPS) * g + b


def _mod_spec(d, n_ctx_tiles, ctx_row):
    return pl.BlockSpec((None, 1, 6 * d), lambda b, i: (jnp.where(i < n_ctx_tiles, ctx_row, b), 0, 0))


def _mods_kernel(c_ref, w_ref, b_ref, o_ref, *, n_rows):
    w = w_ref[...]
    tn = w.shape[1]
    rows = []
    for r in range(n_rows):
        cr = c_ref[r]
        s = cr * jax.nn.sigmoid(cr)
        pieces = [jnp.sum(w[:, j * LANES:(j + 1) * LANES] * s, axis=0, keepdims=True)
                  for j in range(tn // LANES)]
        rows.append(jnp.concatenate(pieces, axis=1))
    rows.append(jnp.zeros((SUBLANES - n_rows, tn), F32))
    o_ref[...] = jnp.concatenate(rows, axis=0) + b_ref[...]


def _mods_call(c_rows, ada_w, ada_b):
    depth, d, six_d = ada_w.shape
    n_rows = c_rows.shape[0]
    tn = 1536 if six_d % 1536 == 0 else six_d
    c_b = jnp.broadcast_to(c_rows[:, :, None], (n_rows, d, LANES))
    return pl.pallas_call(
        functools.partial(_mods_kernel, n_rows=n_rows), name="ada_mods",
        grid=(depth, six_d // tn),
        in_specs=[pl.BlockSpec((n_rows, d, LANES), lambda l, j: (0, 0, 0)),
                  pl.BlockSpec((None, d, tn), lambda l, j: (l, 0, j)),
                  pl.BlockSpec((None, 1, tn), lambda l, j: (l, 0, j))],
        out_specs=pl.BlockSpec((None, SUBLANES, tn), lambda l, j: (l, 0, j)),
        out_shape=jax.ShapeDtypeStruct((depth, SUBLANES, six_d), F32),
        compiler_params=_params("arbitrary", "arbitrary"),
    )(c_b, ada_w, ada_b[:, None, :])


def _inproj_kernel(x_ref, mod_ref, w_ref, b_ref, qg_ref, kg_ref, cos_ref, sin_ref,
                   q_ref, k_ref, v_ref, u_ref, *, d, attn, kvd, conv):
    x = x_ref[...]
    t = x.shape[0]
    h = (x * (1.0 + mod_ref[:, d:2 * d]) + mod_ref[:, 0:d]).astype(BF16)
    p = jnp.dot(h, w_ref[...], preferred_element_type=F32) + b_ref[...]
    cos = cos_ref[...]
    sin = sin_ref[...]
    lane = lax.broadcasted_iota(jnp.int32, (t, LANES), 1)
    even = (lane % 2) == 0
    r_i = lax.broadcasted_iota(jnp.int32, (LANES, LANES), 0) // HEAD_DIM
    c_i = lax.broadcasted_iota(jnp.int32, (LANES, LANES), 1) // HEAD_DIM
    seg = (r_i == c_i).astype(BF16)

    def norm_rope(chunk, g):
        sq = chunk * chunk
        hi = sq.astype(BF16)
        lo = (sq - hi.astype(F32)).astype(BF16)
        ms = (jnp.dot(hi, seg, preferred_element_type=F32)
              + jnp.dot(lo, seg, preferred_element_type=F32)) * (1.0 / HEAD_DIM)
        y = chunk * lax.rsqrt(ms + NORM_EPS) * g
        sw = jnp.where(even, pltpu.roll(y, LANES - 1, 1), pltpu.roll(y, 1, 1))
        return y * cos + sw * sin

    scale = HEAD_DIM ** -0.5 * LOG2_E
    for j in range(attn // LANES):
        sl = slice(j * LANES, (j + 1) * LANES)
        q_ref[:, sl] = (norm_rope(p[:, sl], qg_ref[...]) * scale).astype(BF16)
    for j in range(kvd // LANES):
        sl = slice(attn + j * LANES, attn + (j + 1) * LANES)
        k_ref[:, j * LANES:(j + 1) * LANES] = norm_rope(p[:, sl], kg_ref[...]).astype(BF16)
    v_ref[...] = p[:, attn + kvd:attn + 2 * kvd].astype(BF16)
    a = p[:, attn + 2 * kvd:attn + 2 * kvd + conv]
    g = p[:, attn + 2 * kvd + conv:attn + 2 * kvd + 2 * conv]
    u_ref[...] = a * jax.nn.sigmoid(g)


def _inproj_call(xs, modl, w_in, b_in, qg, kg, cos_t, sin_t, *, nb, s_len, n_ctx_tiles, conv):
    n, d = xs.shape
    in_dim = w_in.shape[1]
    attn = N_Q_HEADS * HEAD_DIM
    kvd = N_KV_HEADS * HEAD_DIM
    nt = s_len // ROW_TILE
    row = lambda b, i: (b * nt + i, 0)
    const = lambda b, i: (0, 0)
    return pl.pallas_call(
        functools.partial(_inproj_kernel, d=d, attn=attn, kvd=kvd, conv=conv), name="attn_inproj",
        grid=(nb, nt),
        in_specs=[pl.BlockSpec((ROW_TILE, d), row),
                  _mod_spec(d, n_ctx_tiles, nb),
                  pl.BlockSpec((d, in_dim), const),
                  pl.BlockSpec((1, in_dim), const),
                  pl.BlockSpec((1, LANES), const),
                  pl.BlockSpec((1, LANES), const),
                  pl.BlockSpec((ROW_TILE, LANES), lambda b, i: (i, 0)),
                  pl.BlockSpec((ROW_TILE, LANES), lambda b, i: (i, 0))],
        out_specs=[pl.BlockSpec((ROW_TILE, attn), row),
                   pl.BlockSpec((ROW_TILE, kvd), row),
                   pl.BlockSpec((ROW_TILE, kvd), row),
                   pl.BlockSpec((ROW_TILE, conv), row)],
        out_shape=[jax.ShapeDtypeStruct((n, attn), BF16),
                   jax.ShapeDtypeStruct((n, kvd), BF16),
                   jax.ShapeDtypeStruct((n, kvd), BF16),
                   jax.ShapeDtypeStruct((n, conv), F32)],
        compiler_params=_params("arbitrary", "arbitrary"),
    )(xs, modl, w_in, b_in, qg, kg, cos_t, sin_t)


def _attn_kernel(q_ref, k_ref, vt_ref, o_ref, m_ref, l_ref, acc_ref, *, grp, bounded):
    kt = pl.program_id(3)

    @pl.when(kt == 0)
    def _():
        m_ref[...] = jnp.full(m_ref.shape, -jnp.inf, F32)
        l_ref[...] = jnp.zeros(l_ref.shape, F32)
        acc_ref[...] = jnp.zeros(acc_ref.shape, F32)

    k = k_ref[...]
    vt = vt_ref[...]
    for g in range(grp):
        s = lax.dot_general(k, q_ref[g], (((1,), (1,)), ((), ())), preferred_element_type=F32)
        if bounded:
            p = jnp.exp2(s)
            l_ref[g] += jnp.sum(p, axis=0, keepdims=True)
            acc_ref[g] += jnp.dot(vt, p.astype(BF16), preferred_element_type=F32)
        else:
            m_prev = m_ref[g]
            m_new = jnp.maximum(m_prev, jnp.max(s, axis=0, keepdims=True))
            alpha = jnp.exp2(m_prev - m_new)
            p = jnp.exp2(s - m_new)
            l_ref[g] = alpha * l_ref[g] + jnp.sum(p, axis=0, keepdims=True)
            acc_ref[g] = alpha * acc_ref[g] + jnp.dot(vt, p.astype(BF16), preferred_element_type=F32)
            m_ref[g] = m_new

    @pl.when(kt == pl.num_programs(3) - 1)
    def _():
        for g in range(grp):
            o_ref[g] = (acc_ref[g] / l_ref[g]).astype(o_ref.dtype)


def _pick_tile(n, candidates):
    for c in candidates:
        if n % c == 0:
            return c
    return n


def _attn_call(qh, kh, vth, *, n_k, bounded):
    nb, n_heads, n_q, dh = qh.shape
    n_kv = kh.shape[1]
    grp = n_heads // n_kv
    tq = _pick_tile(n_q, (512, 256, 128))
    tk = _pick_tile(n_k, (768, 512, 256, 128))
    return pl.pallas_call(
        functools.partial(_attn_kernel, grp=grp, bounded=bounded),
        name="gqa_attn_bounded" if bounded else "gqa_attn",
        grid=(nb, n_kv, n_q // tq, n_k // tk),
        in_specs=[pl.BlockSpec((None, grp, tq, dh), lambda b, h, i, j: (b, h, i, 0)),
                  pl.BlockSpec((None, None, tk, dh), lambda b, h, i, j: (b, h, j, 0)),
                  pl.BlockSpec((None, None, dh, tk), lambda b, h, i, j: (b, h, 0, j))],
        out_specs=pl.BlockSpec((None, grp, dh, tq), lambda b, h, i, j: (b, h, 0, i)),
        out_shape=jax.ShapeDtypeStruct((nb, n_heads, dh, n_q), BF16),
        scratch_shapes=[pltpu.VMEM((grp, 1, tq), F32),
                        pltpu.VMEM((grp, 1, tq), F32),
                        pltpu.VMEM((grp, dh, tq), F32)],
        compiler_params=_params("arbitrary", "arbitrary", "arbitrary", "arbitrary"),
    )(qh, kh, vth)


def _conv_kernel(prev_ref, cur_ref, next_ref, w_ref, b_ref, g_ref, beta_ref, o_ref, ext_ref,
                 *, n_ctx_tiles):
    i = pl.program_id(1)
    nt = pl.num_programs(1)
    t = cur_ref.shape[0]
    first = jnp.logical_or(i == 0, i == n_ctx_tiles)
    last = jnp.logical_or(i == n_ctx_tiles - 1, i == nt - 1)
    ext_ref[0:CONV_HALO, :] = jnp.where(first, 0.0, prev_ref[...])
    ext_ref[CONV_HALO:CONV_HALO + t, :] = cur_ref[...]
    ext_ref[CONV_HALO + t:, :] = jnp.where(last, 0.0, next_ref[...])
    pad = CONV_WIDTH // 2
    acc = jnp.zeros(cur_ref.shape, F32)
    for k in range(CONV_WIDTH):
        acc = acc + w_ref[k:k + 1, :] * ext_ref[pl.ds(CONV_HALO - pad + k, t), :]
    y = _layer_norm(acc + b_ref[...], g_ref[...], beta_ref[...])
    o_ref[...] = (y * jax.nn.sigmoid(y)).astype(o_ref.dtype)


def _conv_call(u, w, bias, ln_g, ln_b, *, nb, s_len, n_ctx_tiles):
    n, c = u.shape
    nt = s_len // ROW_TILE
    per = ROW_TILE // CONV_HALO
    last_halo = n // CONV_HALO - 1
    const = lambda b, i: (0, 0)
    return pl.pallas_call(
        functools.partial(_conv_kernel, n_ctx_tiles=n_ctx_tiles), name="conformer_conv",
        grid=(nb, nt),
        in_specs=[pl.BlockSpec((CONV_HALO, c), lambda b, i: (jnp.maximum((b * nt + i) * per - 1, 0), 0)),
                  pl.BlockSpec((ROW_TILE, c), lambda b, i: (b * nt + i, 0)),
                  pl.BlockSpec((CONV_HALO, c), lambda b, i: (jnp.minimum((b * nt + i + 1) * per, last_halo), 0)),
                  pl.BlockSpec((CONV_WIDTH, c), const),
                  pl.BlockSpec((1, c), const),
                  pl.BlockSpec((1, c), const),
                  pl.BlockSpec((1, c), const)],
        out_specs=pl.BlockSpec((ROW_TILE, c), lambda b, i: (b * nt + i, 0)),
        out_shape=jax.ShapeDtypeStruct((n, c), BF16),
        scratch_shapes=[pltpu.VMEM((ROW_TILE + 2 * CONV_HALO, c), F32)],
        compiler_params=_params("arbitrary", "arbitrary"),
    )(u, u, u, w, bias, ln_g, ln_b)


def _split3(v):
    h1 = v.astype(BF16)
    r1 = v - h1.astype(F32)
    h2 = r1.astype(BF16)
    h3 = (r1 - h2.astype(F32)).astype(BF16)
    return h1, h2, h3


def _mixout_kernel(lhs_ref, w_ref, b_ref, x_ref, mod_ref, g_ref, beta_ref, wr_ref,
                   x1_ref, h2_ref, lg_ref, *, d, alpha, n_exp):
    y = jnp.dot(lhs_ref[...], w_ref[...], preferred_element_type=F32) + b_ref[...]
    x1 = _layer_norm(alpha * x_ref[...] + mod_ref[:, 2 * d:3 * d] * y, g_ref[...], beta_ref[...])
    x1_ref[...] = x1
    h2 = x1 * (1.0 + mod_ref[:, 4 * d:5 * d]) + mod_ref[:, 3 * d:4 * d]
    h2_ref[...] = h2
    a1, a2, a3 = _split3(h2)
    wr = wr_ref[...]
    p1 = jnp.dot(a1, wr, preferred_element_type=F32)
    p2 = jnp.dot(a2, wr, preferred_element_type=F32)
    p3 = jnp.dot(a3, wr, preferred_element_type=F32)
    e = n_exp
    lg_ref[...] = ((p1[:, 2 * e:3 * e] + p2[:, e:2 * e] + p3[:, 0:e])
                   + (p1[:, e:2 * e] + p2[:, 0:e]) + p1[:, 0:e])


def _mixout_call(lhs, w, bias, xs, modl, ln_g, ln_b, wr3, *, nb, s_len, n_ctx_tiles, alpha):
    n, d = xs.shape
    kdim = lhs.shape[1]
    n_exp = wr3.shape[1] // 3
    nt = s_len // ROW_TILE
    row = lambda b, i: (b * nt + i, 0)
    const = lambda b, i: (0, 0)
    return pl.pallas_call(
        functools.partial(_mixout_kernel, d=d, alpha=alpha, n_exp=n_exp), name="mixer_out_norm",
        grid=(nb, nt),
        in_specs=[pl.BlockSpec((ROW_TILE, kdim), row),
                  pl.BlockSpec((kdim, d), const),
                  pl.BlockSpec((1, d), const),
                  pl.BlockSpec((ROW_TILE, d), row),
                  _mod_spec(d, n_ctx_tiles, nb),
                  pl.BlockSpec((1, d), const),
                  pl.BlockSpec((1, d), const),
                  pl.BlockSpec((d, 3 * n_exp), const)],
        out_specs=[pl.BlockSpec((ROW_TILE, d), row),
                   pl.BlockSpec((ROW_TILE, d), row),
                   pl.BlockSpec((ROW_TILE, n_exp), row)],
        out_shape=[jax.ShapeDtypeStruct((n, d), F32),
                   jax.ShapeDtypeStruct((n, d), F32),
                   jax.ShapeDtypeStruct((n, n_exp), F32)],
        compiler_params=_params("arbitrary", "arbitrary"),
    )(lhs, w, bias, xs, modl, ln_g, ln_b, wr3)


def _chan_dft_kernel(x_ref, mod_ref, w_ref, o_ref, *, d):
    h = (x_ref[...] * (1.0 + mod_ref[:, d:2 * d]) + mod_ref[:, 0:d]).astype(BF16)
    o_ref[...] = jnp.dot(h, w_ref[...], preferred_element_type=F32).astype(o_ref.dtype)


def _chan_dft_call(xs, modl, wc, *, nb, s_len, n_ctx_tiles):
    n, d = xs.shape
    nt = s_len // ROW_TILE
    row = lambda b, i: (b * nt + i, 0)
    return pl.pallas_call(
        functools.partial(_chan_dft_kernel, d=d), name="chan_dft",
        grid=(nb, nt),
        in_specs=[pl.BlockSpec((ROW_TILE, d), row),
                  _mod_spec(d, n_ctx_tiles, nb),
                  pl.BlockSpec((d, 2 * d), lambda b, i: (0, 0))],
        out_specs=pl.BlockSpec((ROW_TILE, 2 * d), row),
        out_shape=jax.ShapeDtypeStruct((n, 2 * d), BF16),
        compiler_params=_params("arbitrary", "arbitrary"),
    )(xs, modl, wc)


def _dft_stage1_kernel(h_ref, f_ref, ct_ref, st_ref, o_ref, *, d, n1, group):
    f = f_ref[...]
    for j in range(group):
        p = jnp.dot(f, h_ref[j], preferred_element_type=F32)
        zr = p[0:n1, 0:d] - p[n1:2 * n1, d:2 * d]
        zi = p[n1:2 * n1, 0:d] + p[0:n1, d:2 * d]
        ct = ct_ref[j]
        st = st_ref[j]
        for c in range(d // LANES):
            sl = slice(c * LANES, (c + 1) * LANES)
            o_ref[j, :, sl] = (zr[:, sl] * ct - zi[:, sl] * st).astype(o_ref.dtype)
            o_ref[j, :, d + c * LANES:d + (c + 1) * LANES] = (zr[:, sl] * st + zi[:, sl] * ct).astype(o_ref.dtype)


def _dft_stage1_call(ht, f1, ct, st):
    nb, n2, n1, two_d = ht.shape
    d = two_d // 2
    group = SUBLANES
    return pl.pallas_call(
        functools.partial(_dft_stage1_kernel, d=d, n1=n1, group=group), name="pos_dft_stage1",
        grid=(nb, n2 // group),
        in_specs=[pl.BlockSpec((None, group, n1, two_d), lambda b, i: (b, i, 0, 0)),
                  pl.BlockSpec((2 * n1, n1), lambda b, i: (0, 0)),
                  pl.BlockSpec((group, n1, LANES), lambda b, i: (i, 0, 0)),
                  pl.BlockSpec((group, n1, LANES), lambda b, i: (i, 0, 0))],
        out_specs=pl.BlockSpec((None, group, n1, two_d), lambda b, i: (b, i, 0, 0)),
        out_shape=jax.ShapeDtypeStruct(ht.shape, BF16),
        compiler_params=_params("arbitrary", "arbitrary"),
    )(ht, f1, ct, st)


def _dft_real_kernel(y_ref, c_ref, s_ref, o_ref, *, d):
    y = y_ref[...]
    o_ref[...] = (jnp.dot(c_ref[...], y[:, 0:d], preferred_element_type=F32)
                  - jnp.dot(s_ref[...], y[:, d:2 * d], preferred_element_type=F32)).astype(o_ref.dtype)


def _dft_real_call(y, cm, sm):
    g, n, two_d = y.shape
    d = two_d // 2
    return pl.pallas_call(
        functools.partial(_dft_real_kernel, d=d), name="pos_dft_real",
        grid=(g,),
        in_specs=[pl.BlockSpec((None, n, two_d), lambda i: (i, 0, 0)),
                  pl.BlockSpec((n, n), lambda i: (0, 0)),
                  pl.BlockSpec((n, n), lambda i: (0, 0))],
        out_specs=pl.BlockSpec((None, n, d), lambda i: (i, 0, 0)),
        out_shape=jax.ShapeDtypeStruct((g, n, d), BF16),
        compiler_params=_params("arbitrary"),
    )(y, cm, sm)


def _dft_constants(d, seq, ctx_len):
    grp = d // N_FOURIER_GROUPS
    j = np.arange(grp)
    ang = 2.0 * np.pi * np.outer(j, j) / grp
    wc = np.zeros((d, 2 * d), np.float64)
    for g in range(N_FOURIER_GROUPS):
        sl = slice(g * grp, (g + 1) * grp)
        wc[sl, sl] = np.cos(ang) / math.sqrt(grp)
        wc[sl, d + g * grp:d + (g + 1) * grp] = np.sin(ang) / math.sqrt(grp)
    n2 = DFT_INNER
    n1 = seq // n2
    a1 = 2.0 * np.pi * np.outer(np.arange(n1), np.arange(n1)) / n1
    f1 = np.concatenate([np.cos(a1), np.sin(a1)], axis=0)
    at = 2.0 * np.pi * np.outer(np.arange(n2), np.arange(n1)) / seq
    ct = np.broadcast_to(np.cos(at)[:, :, None], (n2, n1, LANES))
    st = np.broadcast_to(np.sin(at)[:, :, None], (n2, n1, LANES))
    a2 = 2.0 * np.pi * np.outer(np.arange(n2), np.arange(n2)) / n2
    c2 = np.cos(a2) / math.sqrt(seq)
    s2 = np.sin(a2) / math.sqrt(seq)
    ac = 2.0 * np.pi * np.outer(np.arange(ctx_len), np.arange(ctx_len)) / ctx_len
    cc = np.cos(ac) / math.sqrt(ctx_len)
    sc = np.sin(ac) / math.sqrt(ctx_len)
    b = lambda a: jnp.asarray(a, F32).astype(BF16)
    return dict(wc=b(wc), f1=b(f1), ct=jnp.asarray(ct, F32), st=jnp.asarray(st, F32),
                c2=b(c2), s2=b(s2), cc=b(cc), sc=b(sc), n1=n1, n2=n2)


def _fourier_mix(xs, modl, consts, *, nb, s_len, ctx_len, n_ctx_tiles):
    n, d = xs.shape
    seq = s_len - ctx_len
    n1, n2 = consts["n1"], consts["n2"]
    hc = _chan_dft_call(xs, modl, consts["wc"], nb=nb, s_len=s_len, n_ctx_tiles=n_ctx_tiles)
    hc = hc.reshape(nb, s_len, 2 * d)
    h_ctx = hc[:, :ctx_len]
    h_lat = hc[:, ctx_len:].reshape(nb, n1, n2, 2 * d).transpose(0, 2, 1, 3)
    y = _dft_stage1_call(h_lat, consts["f1"], consts["ct"], consts["st"])
    y = y.transpose(0, 2, 1, 3).reshape(nb * n1, n2, 2 * d)
    r = _dft_real_call(y, consts["c2"], consts["s2"])
    f_lat = r.reshape(nb, n1, n2, d).transpose(0, 2, 1, 3).reshape(nb, seq, d)
    f_ctx = _dft_real_call(h_ctx, consts["cc"], consts["sc"])
    return jnp.concatenate([f_ctx, f_lat], axis=1).reshape(n, d)


def _cumsum_lanes(mask, chunk):
    e, n = mask.shape
    r = lax.broadcasted_iota(jnp.int32, (chunk, chunk), 0)
    c = lax.broadcasted_iota(jnp.int32, (chunk, chunk), 1)
    tri = (r < c).astype(BF16)
    carry = jnp.zeros((e, 1), F32)
    out = []
    for j in range(n // chunk):
        m = mask[:, j * chunk:(j + 1) * chunk]
        out.append(jnp.dot(m.astype(BF16), tri, preferred_element_type=F32) + carry)
        carry = carry + jnp.sum(m, axis=1, keepdims=True)
    return jnp.concatenate(out, axis=1)


def _route_select_kernel(lg_ref, aff_ref, pos_ref, *, cap, chunk):
    lg = lg_ref[...]
    mx = jnp.max(lg, axis=0, keepdims=True)
    ex = jnp.exp(lg - mx)
    aff = ex / jnp.sum(ex, axis=0, keepdims=True)
    aff_ref[...] = aff
    e = lg.shape[0]

    def count_above(thr):
        return jnp.sum((aff > thr).astype(F32), axis=1, keepdims=True)

    def exp_step(i, carry):
        a, b, lo, hi = carry
        m = 0.5 * (a + b)
        thr = jnp.exp2(-m)
        ge = count_above(thr) >= cap
        return (jnp.where(ge, a, m), jnp.where(ge, m, b), jnp.where(ge, thr, lo), jnp.where(ge, hi, thr))

    def lin_step(i, carry):
        lo, hi = carry
        mid = 0.5 * (lo + hi)
        ge = count_above(mid) >= cap
        return jnp.where(ge, mid, lo), jnp.where(ge, hi, mid)

    col = lambda v: jnp.full((e, 1), v, F32)
    _, _, lo, hi = lax.fori_loop(0, EXP_BISECT_STEPS, exp_step, (col(-1.0), col(151.0), col(-1.0), col(2.0)))
    lo, hi = lax.fori_loop(0, VALUE_BISECT_STEPS, lin_step, (lo, hi))
    gt = aff > hi
    eq = jnp.logical_and(aff > lo, jnp.logical_not(gt))
    need = cap - jnp.sum(gt.astype(F32), axis=1, keepdims=True)
    tie_rank = _cumsum_lanes(eq.astype(F32), chunk)
    sel = jnp.logical_or(gt, jnp.logical_and(eq, tie_rank < need))
    pos = _cumsum_lanes(sel.astype(F32), chunk)
    pos_ref[...] = jnp.where(sel, pos, -1.0)


def _route_select_call(lgt, cap):
    nb, e, n = lgt.shape
    chunk = min(n, 256)
    return pl.pallas_call(
        functools.partial(_route_select_kernel, cap=cap, chunk=chunk), name="route_select",
        grid=(nb,),
        in_specs=[pl.BlockSpec((None, e, n), lambda b: (b, 0, 0))],
        out_specs=[pl.BlockSpec((None, e, n), lambda b: (b, 0, 0)),
                   pl.BlockSpec((None, e, n), lambda b: (b, 0, 0))],
        out_shape=[jax.ShapeDtypeStruct((nb, e, n), F32),
                   jax.ShapeDtypeStruct((nb, e, n), F32)],
        compiler_params=_params("arbitrary"),
    )(lgt)


def _route_compact_kernel(pos_ref, affc_ref, o_ref, vals_ref, *, cap, chunk, slot_tile):
    e = pl.program_id(1)
    n = affc_ref.shape[0]
    lane = lax.broadcasted_iota(jnp.int32, affc_ref.shape, 1)
    gate = jnp.sum(jnp.where(lane == e, affc_ref[...], 0.0), axis=1, keepdims=True)
    g1, g2, g3 = _split3(gate)
    tok = lax.broadcasted_iota(jnp.int32, (n, LANES), 0)
    col = lax.broadcasted_iota(jnp.int32, (n, LANES), 1)
    vals = jnp.where(col == 0, (tok // LANES).astype(F32),
                     jnp.where(col == 1, (tok % LANES).astype(F32),
                               jnp.where(col == 2, g1.astype(F32),
                                         jnp.where(col == 3, g2.astype(F32),
                                                   jnp.where(col == 4, g3.astype(F32), 0.0)))))
    vals_ref[...] = vals.astype(BF16)
    o_ref[...] = jnp.zeros(o_ref.shape, F32)
    slot = lax.broadcasted_iota(jnp.int32, (slot_tile, chunk), 0).astype(F32)

    def body(c, carry):
        start = pl.multiple_of(c * chunk, chunk)
        prow = pos_ref[:, pl.ds(start, chunk)]
        v = vals_ref[pl.ds(start, chunk), :]
        for j in range(cap // slot_tile):
            onehot = (prow == slot + float(j * slot_tile)).astype(BF16)
            o_ref[j * slot_tile:(j + 1) * slot_tile, :] += jnp.dot(onehot, v, preferred_element_type=F32)
        return carry

    lax.fori_loop(0, n // chunk, body, 0)


def _route_compact_call(pos, affc, cap):
    nb, e, _, n = pos.shape
    chunk = min(n, 256)
    slot_tile = min(cap, 256)
    return pl.pallas_call(
        functools.partial(_route_compact_kernel, cap=cap, chunk=chunk, slot_tile=slot_tile), name="route_compact",
        grid=(nb, e),
        in_specs=[pl.BlockSpec((None, None, 1, n), lambda b, x: (b, x, 0, 0)),
                  pl.BlockSpec((None, n, e), lambda b, x: (b, 0, 0))],
        out_specs=pl.BlockSpec((None, None, cap, LANES), lambda b, x: (b, x, 0, 0)),
        out_shape=jax.ShapeDtypeStruct((nb, e, cap, LANES), F32),
        scratch_shapes=[pltpu.VMEM((n, LANES), BF16)],
        compiler_params=_params("arbitrary", "arbitrary"),
    )(pos, affc)


def _route(logits_seg, cap, row_base):
    nb, n, e = logits_seg.shape
    aff, pos = _route_select_call(logits_seg.transpose(0, 2, 1), cap)
    packed = _route_compact_call(pos[:, :, None, :], aff.transpose(0, 2, 1), cap)
    tok = (packed[..., 0] * LANES + packed[..., 1]).astype(jnp.int32)
    gate = (packed[..., 2] + packed[..., 3]) + packed[..., 4]
    rows = tok + row_base[:, None, None]
    return (rows.transpose(1, 0, 2).reshape(e, nb * cap),
            gate.transpose(1, 0, 2).reshape(e, nb * cap))


def _row_copy(src, src_row, dst, dst_row, sem):
    return pltpu.make_async_copy(src.at[pl.ds(src_row, 1)], dst.at[pl.ds(dst_row, 1)], sem)


def _expert_kernel(idx_ref, idxn_ref, gate_ref, h_hbm, wg_ref, wu_ref, wd_ref, acc_hbm, out_hbm,
                   xg_ref, r_ref, xb_ref, y_ref, sem, *, m, d, fc, step_rows, unroll):
    del acc_hbm
    e = pl.program_id(0)
    f = pl.program_id(1)
    ne = pl.num_programs(0)
    nf = pl.num_programs(1)

    def for_rows(fn):
        def body(i, carry):
            for u in range(unroll):
                fn(i * unroll + u)
            return carry
        lax.fori_loop(0, m // unroll, body, 0)

    def wait_all(buf, s):
        pltpu.make_async_copy(buf, buf, sem.at[s]).wait()

    @pl.when(jnp.logical_and(e == 0, f == 0))
    def _():
        for_rows(lambda j: _row_copy(h_hbm, idx_ref[0, j], xg_ref, j, sem.at[0]).start())

    @pl.when(f == 0)
    def _():
        wait_all(xg_ref, 0)
        xb_ref[...] = xg_ref[...].astype(BF16)
        y_ref[...] = jnp.zeros(y_ref.shape, F32)

    for u in range(step_rows):
        j = f * step_rows + u
        _row_copy(h_hbm, idxn_ref[0, j], xg_ref, j, sem.at[0]).start()
        _row_copy(out_hbm, idx_ref[0, j], r_ref, j, sem.at[1]).start()

    xb = xb_ref[...]
    gate_h = jnp.dot(xb, wg_ref[...].astype(BF16), preferred_element_type=F32)
    up_h = jnp.dot(xb, wu_ref[...].astype(BF16), preferred_element_type=F32)
    hid = gate_h * jax.nn.sigmoid(gate_h) * up_h
    g = gate_ref[...]
    hid = jnp.concatenate([hid[:, c * LANES:(c + 1) * LANES] * g for c in range(fc // LANES)], axis=1)
    y_ref[...] += jnp.dot(hid.astype(BF16), wd_ref[...].astype(BF16), preferred_element_type=F32)

    @pl.when(f == nf - 1)
    def _():
        wait_all(r_ref, 1)
        r_ref[...] += y_ref[...]
        for_rows(lambda j: _row_copy(r_ref, j, out_hbm, idx_ref[0, j], sem.at[2]).start())
        wait_all(r_ref, 2)

    @pl.when(jnp.logical_and(e == ne - 1, f == nf - 1))
    def _():
        wait_all(xg_ref, 0)


def _expert_call(idx, gates_b, h2, w_gate, w_up, w_down, acc0, *, layer):
    n_exp, m = idx.shape
    n, d = h2.shape
    ff = w_gate.shape[3]
    fc = 256 if ff % 256 == 0 else ff
    nf = ff // fc
    unroll = 8
    assert m % nf == 0 and m % unroll == 0 and fc % LANES == 0
    nxt = lambda e, f: (jnp.minimum(e + 1, n_exp - 1), 0, 0)
    grid_spec = pl.GridSpec(
        grid=(n_exp, nf),
        in_specs=[pl.BlockSpec((None, 1, m), lambda e, f: (e, 0, 0), memory_space=pltpu.SMEM),
                  pl.BlockSpec((None, 1, m), nxt, memory_space=pltpu.SMEM),
                  pl.BlockSpec((None, m, LANES), lambda e, f: (e, 0, 0)),
                  pl.BlockSpec(memory_space=pl.ANY),
                  pl.BlockSpec((None, None, d, fc), lambda e, f: (layer, e, 0, f)),
                  pl.BlockSpec((None, None, d, fc), lambda e, f: (layer, e, 0, f)),
                  pl.BlockSpec((None, None, fc, d), lambda e, f: (layer, e, f, 0)),
                  pl.BlockSpec(memory_space=pl.ANY)],
        out_specs=pl.BlockSpec(memory_space=pl.ANY),
        scratch_shapes=[pltpu.VMEM((m, d), F32),
                        pltpu.VMEM((m, d), F32),
                        pltpu.VMEM((m, d), BF16),
                        pltpu.VMEM((m, d), F32),
                        pltpu.SemaphoreType.DMA((3,))],
    )
    idx3 = idx[:, None, :]
    return pl.pallas_call(
        functools.partial(_expert_kernel, m=m, d=d, fc=fc, step_rows=m // nf, unroll=unroll), name="moe_experts",
        grid_spec=grid_spec,
        out_shape=jax.ShapeDtypeStruct((n, d), F32),
        input_output_aliases={7: 0},
        compiler_params=_params("arbitrary", "arbitrary"),
    )(idx3, idx3, gates_b, h2, w_gate, w_up, w_down, acc0)


def _moe_norm_kernel(x_ref, y_ref, mod_ref, g_ref, beta_ref, o_ref, *, d, alpha):
    o_ref[...] = _layer_norm(alpha * x_ref[...] + mod_ref[:, 5 * d:6 * d] * y_ref[...],
                             g_ref[...], beta_ref[...])


def _moe_norm_call(x1, y, modl, ln_g, ln_b, *, nb, s_len, n_ctx_tiles, alpha):
    n, d = x1.shape
    nt = s_len // ROW_TILE
    row = lambda b, i: (b * nt + i, 0)
    const = lambda b, i: (0, 0)
    return pl.pallas_call(
        functools.partial(_moe_norm_kernel, d=d, alpha=alpha), name="moe_norm",
        grid=(nb, nt),
        in_specs=[pl.BlockSpec((ROW_TILE, d), row),
                  pl.BlockSpec((ROW_TILE, d), row),
                  _mod_spec(d, n_ctx_tiles, nb),
                  pl.BlockSpec((1, d), const),
                  pl.BlockSpec((1, d), const)],
        out_specs=pl.BlockSpec((ROW_TILE, d), row),
        out_shape=jax.ShapeDtypeStruct((n, d), F32),
        compiler_params=_params("arbitrary", "arbitrary"),
    )(x1, y, modl, ln_g, ln_b)


def _rope_tables(seq, ctx_len):
    axis_dim = HEAD_DIM // 2
    rows = seq // GRID_W
    row = jnp.repeat(jnp.arange(rows, dtype=F32), GRID_W)
    col = jnp.tile(jnp.arange(GRID_W, dtype=F32), rows)
    inv_freq = ROPE_THETA ** (-jnp.arange(0, axis_dim, 2, dtype=F32) / axis_dim)
    ang = jnp.concatenate([row[:, None] * inv_freq, col[:, None] * inv_freq], axis=-1)
    cos = jnp.repeat(jnp.cos(ang), 2, axis=1)
    sin = jnp.repeat(jnp.sin(ang), 2, axis=1) * jnp.tile(jnp.array([-1.0, 1.0], F32), HEAD_DIM // 2)
    cos = jnp.concatenate([jnp.ones((ctx_len, HEAD_DIM), F32), cos], axis=0)
    sin = jnp.concatenate([jnp.zeros((ctx_len, HEAD_DIM), F32), sin], axis=0)
    rep = LANES // HEAD_DIM
    return jnp.tile(cos, (1, rep)), jnp.tile(sin, (1, rep))


def _split3_host(w):
    w1 = w.astype(BF16)
    r1 = w - w1.astype(F32)
    w2 = r1.astype(BF16)
    w3 = (r1 - w2.astype(F32)).astype(BF16)
    return jnp.concatenate([w1, w2, w3], axis=1)


def kernel(x, c, ctx, c_ctx, ada_w, ada_b, ln_g, ln_b, attn_in_w, attn_in_b, q_norm_g, k_norm_g, conv_w, conv_b, conv_ln_g, conv_ln_b, attn_out_w, attn_out_b, fourier_out_w, fourier_out_b, router_w, expert_w_gate, expert_w_up, expert_w_down):
    nb, seq, d = x.shape
    ctx_len = ctx.shape[1]
    depth = ada_w.shape[0]
    n_exp = router_w.shape[2]
    conv = conv_w.shape[2]
    attn = N_Q_HEADS * HEAD_DIM
    kvd = N_KV_HEADS * HEAD_DIM
    s_len = ctx_len + seq
    n = nb * s_len
    assert ctx_len % ROW_TILE == 0 and seq % ROW_TILE == 0 and seq % DFT_INNER == 0
    assert nb + 1 <= SUBLANES and LANES % HEAD_DIM == 0 and attn % LANES == 0 and kvd % LANES == 0
    n_ctx_tiles = ctx_len // ROW_TILE
    alpha = (2 * depth) ** 0.25
    cap_lat = CAPACITY_FACTOR * seq // n_exp
    cap_ctx = CAPACITY_FACTOR * ctx_len // n_exp
    tile_kw = dict(nb=nb, s_len=s_len, n_ctx_tiles=n_ctx_tiles)

    mods = _mods_call(jnp.concatenate([c, c_ctx[None, :]], axis=0), ada_w, ada_b)
    cos_t, sin_t = _rope_tables(seq, ctx_len)
    consts = _dft_constants(d, seq, ctx_len)
    rep = LANES // HEAD_DIM
    base = jnp.arange(nb, dtype=jnp.int32) * s_len

    xs = jnp.concatenate([ctx, x], axis=1).reshape(n, d)
    for i in range(depth):
        j = i // 2
        modl = mods[i][:, None, :]
        if i % 2 == 0:
            q, k, v, u = _inproj_call(
                xs, modl, attn_in_w[j].astype(BF16), attn_in_b[j][None, :],
                jnp.tile(q_norm_g[j], rep)[None, :], jnp.tile(k_norm_g[j], rep)[None, :],
                cos_t, sin_t, conv=conv, **tile_kw)
            qh = q.reshape(nb, s_len, N_Q_HEADS, HEAD_DIM).transpose(0, 2, 1, 3)
            kh = k.reshape(nb, s_len, N_KV_HEADS, HEAD_DIM).transpose(0, 2, 1, 3)
            vth = v.reshape(nb, s_len, N_KV_HEADS, HEAD_DIM).transpose(0, 2, 3, 1)
            score_bound = (1.02 * HEAD_DIM ** 0.5 * LOG2_E) * jnp.max(jnp.abs(q_norm_g[j])) * jnp.max(jnp.abs(k_norm_g[j]))
            q_lat, q_ctx = qh[:, :, ctx_len:], qh[:, :, :ctx_len]

            def attend(bounded):
                return lambda: (_attn_call(q_lat, kh, vth, n_k=s_len, bounded=bounded),
                                _attn_call(q_ctx, kh, vth, n_k=ctx_len, bounded=bounded))

            o_lat, o_ctx = lax.cond(score_bound < MAX_UNSHIFTED_LOG2_SCORE, attend(True), attend(False))
            att = jnp.concatenate([o_ctx, o_lat], axis=3).transpose(0, 3, 1, 2).reshape(n, attn)
            cv = _conv_call(u, conv_w[j], conv_b[j][None, :], conv_ln_g[j][None, :], conv_ln_b[j][None, :],
                            **tile_kw)
            lhs = jnp.concatenate([att, cv], axis=1)
            w_o, b_o = attn_out_w[j], attn_out_b[j]
        else:
            lhs = _fourier_mix(xs, modl, consts, nb=nb, s_len=s_len, ctx_len=ctx_len, n_ctx_tiles=n_ctx_tiles)
            w_o, b_o = fourier_out_w[j], fourier_out_b[j]
        x1, h2, logits = _mixout_call(lhs, w_o.astype(BF16), b_o[None, :], xs, modl,
                                      ln_g[i, 0][None, :], ln_b[i, 0][None, :], _split3_host(router_w[i]),
                                      alpha=alpha, **tile_kw)
        lg = logits.reshape(nb, s_len, n_exp)
        rows_l, gates_l = _route(lg[:, ctx_len:], cap_lat, base + ctx_len)
        rows_c, gates_c = _route(lg[:, :ctx_len], cap_ctx, base)
        idx = jnp.concatenate([rows_l, rows_c], axis=1)
        gates = jnp.concatenate([gates_l, gates_c], axis=1)
        gates_b = jnp.broadcast_to(gates[:, :, None], gates.shape + (LANES,))
        y = _expert_call(idx, gates_b, h2, expert_w_gate, expert_w_up, expert_w_down,
                         jnp.zeros((n, d), F32), layer=i)
        xs = _moe_norm_call(x1, y, modl, ln_g[i, 1][None, :], ln_b[i, 1][None, :], alpha=alpha, **tile_kw)
    return xs.reshape(nb, s_len, d)[:, ctx_len:]
```

```python
import functools
import math

import numpy as np
import jax
import jax.numpy as jnp
from jax import lax
from jax.experimental import pallas as pl
from jax.experimental.pallas import tpu as pltpu

GRID_W = 64
N_Q_HEADS = 8
N_KV_HEADS = 2
HEAD_DIM = 64
ROPE_THETA = 10000.0
CONV_WIDTH = 31
N_FOURIER_GROUPS = 4
CAPACITY_FACTOR = 2
NORM_EPS = 1e-6

LANES = 128
SUBLANES = 8
VMEM_BYTES_V7X = 64 * 1024 * 1024
VMEM_LIMIT = VMEM_BYTES_V7X - 8 * 1024 * 1024

ROW_TILE = 256
CONV_HALO = 16
DFT_INNER = 128
LOG2_E = math.log2(math.e)
MAX_UNSHIFTED_LOG2_SCORE = 96.0
EXP_BISECT_STEPS = 9
VALUE_BISECT_STEPS = 32

F32 = jnp.float32
BF16 = jnp.bfloat16


def _params(*sem):
    return pltpu.CompilerParams(dimension_semantics=sem, vmem_limit_bytes=VMEM_LIMIT)


def _layer_norm(v, g, b):
    mu = jnp.mean(v, axis=-1, keepdims=True)
    d = v - mu
    var = jnp.mean(d * d, axis=-1, keepdims=True)
    return d * lax.rsqrt(var + NORM_EPS) * g + b


def _mod_spec(d, n_ctx_tiles, ctx_row):
    return pl.BlockSpec((None, 1, 6 * d), lambda b, i: (jnp.where(i < n_ctx_tiles, ctx_row, b), 0, 0))


def _mods_kernel(c_ref, w_ref, b_ref, o_ref, *, n_rows):
    w = w_ref[...]
    tn = w.shape[1]
    rows = []
    for r in range(n_rows):
        cr = c_ref[r]
        s = cr * jax.nn.sigmoid(cr)
        pieces = [jnp.sum(w[:, j * LANES:(j + 1) * LANES] * s, axis=0, keepdims=True)
                  for j in range(tn // LANES)]
        rows.append(jnp.concatenate(pieces, axis=1))
    rows.append(jnp.zeros((SUBLANES - n_rows, tn), F32))
    o_ref[...] = jnp.concatenate(rows, axis=0) + b_ref[...]


def _mods_call(c_rows, ada_w, ada_b):
    depth, d, six_d = ada_w.shape
    n_rows = c_rows.shape[0]
    tn = 1536 if six_d % 1536 == 0 else six_d
    c_b = jnp.broadcast_to(c_rows[:, :, None], (n_rows, d, LANES))
    return pl.pallas_call(
        functools.partial(_mods_kernel, n_rows=n_rows), name="ada_mods",
        grid=(depth, six_d // tn),
        in_specs=[pl.BlockSpec((n_rows, d, LANES), lambda l, j: (0, 0, 0)),
                  pl.BlockSpec((None, d, tn), lambda l, j: (l, 0, j)),
                  pl.BlockSpec((None, 1, tn), lambda l, j: (l, 0, j))],
        out_specs=pl.BlockSpec((None, SUBLANES, tn), lambda l, j: (l, 0, j)),
        out_shape=jax.ShapeDtypeStruct((depth, SUBLANES, six_d), F32),
        compiler_params=_params("arbitrary", "arbitrary"),
    )(c_b, ada_w, ada_b[:, None, :])


def _inproj_kernel(x_ref, mod_ref, w_ref, b_ref, qg_ref, kg_ref, cos_ref, sin_ref,
                   q_ref, k_ref, v_ref, u_ref, *, d, attn, kvd, conv):
    x = x_ref[...]
    t = x.shape[0]
    h = (x * (1.0 + mod_ref[:, d:2 * d]) + mod_ref[:, 0:d]).astype(BF16)
    p = jnp.dot(h, w_ref[...], preferred_element_type=F32) + b_ref[...]
    cos = cos_ref[...]
    sin = sin_ref[...]
    lane = lax.broadcasted_iota(jnp.int32, (t, LANES), 1)
    even = (lane % 2) == 0
    r_i = lax.broadcasted_iota(jnp.int32, (LANES, LANES), 0) // HEAD_DIM
    c_i = lax.broadcasted_iota(jnp.int32, (LANES, LANES), 1) // HEAD_DIM
    seg = (r_i == c_i).astype(BF16)

    def norm_rope(chunk, g):
        sq = chunk * chunk
        hi = sq.astype(BF16)
        lo = (sq - hi.astype(F32)).astype(BF16)
        ms = (jnp.dot(hi, seg, preferred_element_type=F32)
              + jnp.dot(lo, seg, preferred_element_type=F32)) * (1.0 / HEAD_DIM)
        y = chunk * lax.rsqrt(ms + NORM_EPS) * g
        sw = jnp.where(even, pltpu.roll(y, LANES - 1, 1), pltpu.roll(y, 1, 1))
        return y * cos + sw * sin

    scale = HEAD_DIM ** -0.5 * LOG2_E
    for j in range(attn // LANES):
        sl = slice(j * LANES, (j + 1) * LANES)
        q_ref[:, sl] = (norm_rope(p[:, sl], qg_ref[...]) * scale).astype(BF16)
    for j in range(kvd // LANES):
        sl = slice(attn + j * LANES, attn + (j + 1) * LANES)
        k_ref[:, j * LANES:(j + 1) * LANES] = norm_rope(p[:, sl], kg_ref[...]).astype(BF16)
    v_ref[...] = p[:, attn + kvd:attn + 2 * kvd].astype(BF16)
    a = p[:, attn + 2 * kvd:attn + 2 * kvd + conv]
    g = p[:, attn + 2 * kvd + conv:attn + 2 * kvd + 2 * conv]
    u_ref[...] = a * jax.nn.sigmoid(g)


def _inproj_call(xs, modl, w_in, b_in, qg, kg, cos_t, sin_t, *, nb, s_len, n_ctx_tiles, conv):
    n, d = xs.shape
    in_dim = w_in.shape[1]
    attn = N_Q_HEADS * HEAD_DIM
    kvd = N_KV_HEADS * HEAD_DIM
    nt = s_len // ROW_TILE
    row = lambda b, i: (b * nt + i, 0)
    const = lambda b, i: (0, 0)
    return pl.pallas_call(
        functools.partial(_inproj_kernel, d=d, attn=attn, kvd=kvd, conv=conv), name="attn_inproj",
        grid=(nb, nt),
        in_specs=[pl.BlockSpec((ROW_TILE, d), row),
                  _mod_spec(d, n_ctx_tiles, nb),
                  pl.BlockSpec((d, in_dim), const),
                  pl.BlockSpec((1, in_dim), const),
                  pl.BlockSpec((1, LANES), const),
                  pl.BlockSpec((1, LANES), const),
                  pl.BlockSpec((ROW_TILE, LANES), lambda b, i: (i, 0)),
                  pl.BlockSpec((ROW_TILE, LANES), lambda b, i: (i, 0))],
        out_specs=[pl.BlockSpec((ROW_TILE, attn), row),
                   pl.BlockSpec((ROW_TILE, kvd), row),
                   pl.BlockSpec((ROW_TILE, kvd), row),
                   pl.BlockSpec((ROW_TILE, conv), row)],
        out_shape=[jax.ShapeDtypeStruct((n, attn), BF16),
                   jax.ShapeDtypeStruct((n, kvd), BF16),
                   jax.ShapeDtypeStruct((n, kvd), BF16),
                   jax.ShapeDtypeStruct((n, conv), F32)],
        compiler_params=_params("arbitrary", "arbitrary"),
    )(xs, modl, w_in, b_in, qg, kg, cos_t, sin_t)


def _attn_kernel(q_ref, k_ref, vt_ref, o_ref, m_ref, l_ref, acc_ref, *, grp, bounded):
    kt = pl.program_id(3)

    @pl.when(kt == 0)
    def _():
        m_ref[...] = jnp.full(m_ref.shape, -jnp.inf, F32)
        l_ref[...] = jnp.zeros(l_ref.shape, F32)
        acc_ref[...] = jnp.zeros(acc_ref.shape, F32)

    k = k_ref[...]
    vt = vt_ref[...]
    for g in range(grp):
        s = lax.dot_general(k, q_ref[g], (((1,), (1,)), ((), ())), preferred_element_type=F32)
        if bounded:
            p = jnp.exp2(s)
            l_ref[g] += jnp.sum(p, axis=0, keepdims=True)
            acc_ref[g] += jnp.dot(vt, p.astype(BF16), preferred_element_type=F32)
        else:
            m_prev = m_ref[g]
            m_new = jnp.maximum(m_prev, jnp.max(s, axis=0, keepdims=True))
            alpha = jnp.exp2(m_prev - m_new)
            p = jnp.exp2(s - m_new)
            l_ref[g] = alpha * l_ref[g] + jnp.sum(p, axis=0, keepdims=True)
            acc_ref[g] = alpha * acc_ref[g] + jnp.dot(vt, p.astype(BF16), preferred_element_type=F32)
            m_ref[g] = m_new

    @pl.when(kt == pl.num_programs(3) - 1)
    def _():
        for g in range(grp):
            o_ref[g] = (acc_ref[g] / l_ref[g]).astype(o_ref.dtype)


def _pick_tile(n, candidates):
    for c in candidates:
        if n % c == 0:
            return c
    return n


def _attn_call(qh, kh, vth, *, n_k, bounded):
    nb, n_heads, n_q, dh = qh.shape
    n_kv = kh.shape[1]
    grp = n_heads // n_kv
    tq = _pick_tile(n_q, (512, 256, 128))
    tk = _pick_tile(n_k, (768, 512, 256, 128))
    return pl.pallas_call(
        functools.partial(_attn_kernel, grp=grp, bounded=bounded),
        name="gqa_attn_bounded" if bounded else "gqa_attn",
        grid=(nb, n_kv, n_q // tq, n_k // tk),
        in_specs=[pl.BlockSpec((None, grp, tq, dh), lambda b, h, i, j: (b, h, i, 0)),
                  pl.BlockSpec((None, None, tk, dh), lambda b, h, i, j: (b, h, j, 0)),
                  pl.BlockSpec((None, None, dh, tk), lambda b, h, i, j: (b, h, 0, j))],
        out_specs=pl.BlockSpec((None, grp, dh, tq), lambda b, h, i, j: (b, h, 0, i)),
        out_shape=jax.ShapeDtypeStruct((nb, n_heads, dh, n_q), BF16),
        scratch_shapes=[pltpu.VMEM((grp, 1, tq), F32),
                        pltpu.VMEM((grp, 1, tq), F32),
                        pltpu.VMEM((grp, dh, tq), F32)],
        compiler_params=_params("arbitrary", "arbitrary", "arbitrary", "arbitrary"),
    )(qh, kh, vth)


def _conv_kernel(prev_ref, cur_ref, next_ref, w_ref, b_ref, g_ref, beta_ref, o_ref, ext_ref,
                 *, n_ctx_tiles):
    i = pl.program_id(1)
    nt = pl.num_programs(1)
    t = cur_ref.shape[0]
    first = jnp.logical_or(i == 0, i == n_ctx_tiles)
    last = jnp.logical_or(i == n_ctx_tiles - 1, i == nt - 1)
    ext_ref[0:CONV_HALO, :] = jnp.where(first, 0.0, prev_ref[...])
    ext_ref[CONV_HALO:CONV_HALO + t, :] = cur_ref[...]
    ext_ref[CONV_HALO + t:, :] = jnp.where(last, 0.0, next_ref[...])
    pad = CONV_WIDTH // 2
    acc = jnp.zeros(cur_ref.shape, F32)
    for k in range(CONV_WIDTH):
        acc = acc + w_ref[k:k + 1, :] * ext_ref[pl.ds(CONV_HALO - pad + k, t), :]
    y = _layer_norm(acc + b_ref[...], g_ref[...], beta_ref[...])
    o_ref[...] = (y * jax.nn.sigmoid(y)).astype(o_ref.dtype)


def _conv_call(u, w, bias, ln_g, ln_b, *, nb, s_len, n_ctx_tiles):
    n, c = u.shape
    nt = s_len // ROW_TILE
    per = ROW_TILE // CONV_HALO
    last_halo = n // CONV_HALO - 1
    const = lambda b, i: (0, 0)
    return pl.pallas_call(
        functools.partial(_conv_kernel, n_ctx_tiles=n_ctx_tiles), name="conformer_conv",
        grid=(nb, nt),
        in_specs=[pl.BlockSpec((CONV_HALO, c), lambda b, i: (jnp.maximum((b * nt + i) * per - 1, 0), 0)),
                  pl.BlockSpec((ROW_TILE, c), lambda b, i: (b * nt + i, 0)),
                  pl.BlockSpec((CONV_HALO, c), lambda b, i: (jnp.minimum((b * nt + i + 1) * per, last_halo), 0)),
                  pl.BlockSpec((CONV_WIDTH, c), const),
                  pl.BlockSpec((1, c), const),
                  pl.BlockSpec((1, c), const),
                  pl.BlockSpec((1, c), const)],
        out_specs=pl.BlockSpec((ROW_TILE, c), lambda b, i: (b * nt + i, 0)),
        out_shape=jax.ShapeDtypeStruct((n, c), BF16),
        scratch_shapes=[pltpu.VMEM((ROW_TILE + 2 * CONV_HALO, c), F32)],
        compiler_params=_params("arbitrary", "arbitrary"),
    )(u, u, u, w, bias, ln_g, ln_b)


def _split3(v):
    h1 = v.astype(BF16)
    r1 = v - h1.astype(F32)
    h2 = r1.astype(BF16)
    h3 = (r1 - h2.astype(F32)).astype(BF16)
    return h1, h2, h3


def _mixout_kernel(lhs_ref, w_ref, b_ref, x_ref, mod_ref, g_ref, beta_ref, wr_ref,
                   x1_ref, h2_ref, lg_ref, *, d, alpha, n_exp):
    y = jnp.dot(lhs_ref[...], w_ref[...], preferred_element_type=F32) + b_ref[...]
    x1 = _layer_norm(alpha * x_ref[...] + mod_ref[:, 2 * d:3 * d] * y, g_ref[...], beta_ref[...])
    x1_ref[...] = x1
    h2 = x1 * (1.0 + mod_ref[:, 4 * d:5 * d]) + mod_ref[:, 3 * d:4 * d]
    h2_ref[...] = h2
    a1, a2, a3 = _split3(h2)
    wr = wr_ref[...]
    p1 = jnp.dot(a1, wr, preferred_element_type=F32)
    p2 = jnp.dot(a2, wr, preferred_element_type=F32)
    p3 = jnp.dot(a3, wr, preferred_element_type=F32)
    e = n_exp
    lg_ref[...] = ((p1[:, 2 * e:3 * e] + p2[:, e:2 * e] + p3[:, 0:e])
                   + (p1[:, e:2 * e] + p2[:, 0:e]) + p1[:, 0:e])


def _mixout_call(lhs, w, bias, xs, modl, ln_g, ln_b, wr3, *, nb, s_len, n_ctx_tiles, alpha):
    n, d = xs.shape
    kdim = lhs.shape[1]
    n_exp = wr3.shape[1] // 3
    nt = s_len // ROW_TILE
    row = lambda b, i: (b * nt + i, 0)
    const = lambda b, i: (0, 0)
    return pl.pallas_call(
        functools.partial(_mixout_kernel, d=d, alpha=alpha, n_exp=n_exp), name="mixer_out_norm",
        grid=(nb, nt),
        in_specs=[pl.BlockSpec((ROW_TILE, kdim), row),
                  pl.BlockSpec((kdim, d), const),
                  pl.BlockSpec((1, d), const),
                  pl.BlockSpec((ROW_TILE, d), row),
                  _mod_spec(d, n_ctx_tiles, nb),
                  pl.BlockSpec((1, d), const),
                  pl.BlockSpec((1, d), const),
                  pl.BlockSpec((d, 3 * n_exp), const)],
        out_specs=[pl.BlockSpec((ROW_TILE, d), row),
                   pl.BlockSpec((ROW_TILE, d), row),
                   pl.BlockSpec((ROW_TILE, n_exp), row)],
        out_shape=[jax.ShapeDtypeStruct((n, d), F32),
                   jax.ShapeDtypeStruct((n, d), F32),
                   jax.ShapeDtypeStruct((n, n_exp), F32)],
        compiler_params=_params("arbitrary", "arbitrary"),
    )(lhs, w, bias, xs, modl, ln_g, ln_b, wr3)


def _chan_dft_kernel(x_ref, mod_ref, w_ref, o_ref, *, d):
    h = (x_ref[...] * (1.0 + mod_ref[:, d:2 * d]) + mod_ref[:, 0:d]).astype(BF16)
    o_ref[...] = jnp.dot(h, w_ref[...], preferred_element_type=F32).astype(o_ref.dtype)


def _chan_dft_call(xs, modl, wc, *, nb, s_len, n_ctx_tiles):
    n, d = xs.shape
    nt = s_len // ROW_TILE
    row = lambda b, i: (b * nt + i, 0)
    return pl.pallas_call(
        functools.partial(_chan_dft_kernel, d=d), name="chan_dft",
        grid=(nb, nt),
        in_specs=[pl.BlockSpec((ROW_TILE, d), row),
                  _mod_spec(d, n_ctx_tiles, nb),
                  pl.BlockSpec((d, 2 * d), lambda b, i: (0, 0))],
        out_specs=pl.BlockSpec((ROW_TILE, 2 * d), row),
        out_shape=jax.ShapeDtypeStruct((n, 2 * d), BF16),
        compiler_params=_params("arbitrary", "arbitrary"),
    )(xs, modl, wc)


def _dft_stage1_kernel(h_ref, f_ref, ct_ref, st_ref, o_ref, *, d, n1, group):
    f = f_ref[...]
    for j in range(group):
        p = jnp.dot(f, h_ref[j], preferred_element_type=F32)
        zr = p[0:n1, 0:d] - p[n1:2 * n1, d:2 * d]
        zi = p[n1:2 * n1, 0:d] + p[0:n1, d:2 * d]
        ct = ct_ref[j]
        st = st_ref[j]
        for c in range(d // LANES):
            sl = slice(c * LANES, (c + 1) * LANES)
            o_ref[j, :, sl] = (zr[:, sl] * ct - zi[:, sl] * st).astype(o_ref.dtype)
            o_ref[j, :, d + c * LANES:d + (c + 1) * LANES] = (zr[:, sl] * st + zi[:, sl] * ct).astype(o_ref.dtype)


def _dft_stage1_call(ht, f1, ct, st):
    nb, n2, n1, two_d = ht.shape
    d = two_d // 2
    group = SUBLANES
    return pl.pallas_call(
        functools.partial(_dft_stage1_kernel, d=d, n1=n1, group=group), name="pos_dft_stage1",
        grid=(nb, n2 // group),
        in_specs=[pl.BlockSpec((None, group, n1, two_d), lambda b, i: (b, i, 0, 0)),
                  pl.BlockSpec((2 * n1, n1), lambda b, i: (0, 0)),
                  pl.BlockSpec((group, n1, LANES), lambda b, i: (i, 0, 0)),
                  pl.BlockSpec((group, n1, LANES), lambda b, i: (i, 0, 0))],
        out_specs=pl.BlockSpec((None, group, n1, two_d), lambda b, i: (b, i, 0, 0)),
        out_shape=jax.ShapeDtypeStruct(ht.shape, BF16),
        compiler_params=_params("arbitrary", "arbitrary"),
    )(ht, f1, ct, st)


def _dft_real_kernel(y_ref, c_ref, s_ref, o_ref, *, d):
    y = y_ref[...]
    o_ref[...] = (jnp.dot(c_ref[...], y[:, 0:d], preferred_element_type=F32)
                  - jnp.dot(s_ref[...], y[:, d:2 * d], preferred_element_type=F32)).astype(o_ref.dtype)


def _dft_real_call(y, cm, sm):
    g, n, two_d = y.shape
    d = two_d // 2
    return pl.pallas_call(
        functools.partial(_dft_real_kernel, d=d), name="pos_dft_real",
        grid=(g,),
        in_specs=[pl.BlockSpec((None, n, two_d), lambda i: (i, 0, 0)),
                  pl.BlockSpec((n, n), lambda i: (0, 0)),
                  pl.BlockSpec((n, n), lambda i: (0, 0))],
        out_specs=pl.BlockSpec((None, n, d), lambda i: (i, 0, 0)),
        out_shape=jax.ShapeDtypeStruct((g, n, d), BF16),
        compiler_params=_params("arbitrary"),
    )(y, cm, sm)


def _dft_constants(d, seq, ctx_len):
    grp = d // N_FOURIER_GROUPS
    j = np.arange(grp)
    ang = 2.0 * np.pi * np.outer(j, j) / grp
    wc = np.zeros((d, 2 * d), np.float64)
    for g in range(N_FOURIER_GROUPS):
        sl = slice(g * grp, (g + 1) * grp)
        wc[sl, sl] = np.cos(ang) / math.sqrt(grp)
        wc[sl, d + g * grp:d + (g + 1) * grp] = np.sin(ang) / math.sqrt(grp)
    n2 = DFT_INNER
    n1 = seq // n2
    a1 = 2.0 * np.pi * np.outer(np.arange(n1), np.arange(n1)) / n1
    f1 = np.concatenate([np.cos(a1), np.sin(a1)], axis=0)
    at = 2.0 * np.pi * np.outer(np.arange(n2), np.arange(n1)) / seq
    ct = np.broadcast_to(np.cos(at)[:, :, None], (n2, n1, LANES))
    st = np.broadcast_to(np.sin(at)[:, :, None], (n2, n1, LANES))
    a2 = 2.0 * np.pi * np.outer(np.arange(n2), np.arange(n2)) / n2
    c2 = np.cos(a2) / math.sqrt(seq)
    s2 = np.sin(a2) / math.sqrt(seq)
    ac = 2.0 * np.pi * np.outer(np.arange(ctx_len), np.arange(ctx_len)) / ctx_len
    cc = np.cos(ac) / math.sqrt(ctx_len)
    sc = np.sin(ac) / math.sqrt(ctx_len)
    b = lambda a: jnp.asarray(a, F32).astype(BF16)
    return dict(wc=b(wc), f1=b(f1), ct=jnp.asarray(ct, F32), st=jnp.asarray(st, F32),
                c2=b(c2), s2=b(s2), cc=b(cc), sc=b(sc), n1=n1, n2=n2)


def _fourier_mix(xs, modl, consts, *, nb, s_len, ctx_len, n_ctx_tiles):
    n, d = xs.shape
    seq = s_len - ctx_len
    n1, n2 = consts["n1"], consts["n2"]
    hc = _chan_dft_call(xs, modl, consts["wc"], nb=nb, s_len=s_len, n_ctx_tiles=n_ctx_tiles)
    hc = hc.reshape(nb, s_len, 2 * d)
    h_ctx = hc[:, :ctx_len]
    h_lat = hc[:, ctx_len:].reshape(nb, n1, n2, 2 * d).transpose(0, 2, 1, 3)
    y = _dft_stage1_call(h_lat, consts["f1"], consts["ct"], consts["st"])
    y = y.transpose(0, 2, 1, 3).reshape(nb * n1, n2, 2 * d)
    r = _dft_real_call(y, consts["c2"], consts["s2"])
    f_lat = r.reshape(nb, n1, n2, d).transpose(0, 2, 1, 3).reshape(nb, seq, d)
    f_ctx = _dft_real_call(h_ctx, consts["cc"], consts["sc"])
    return jnp.concatenate([f_ctx, f_lat], axis=1).reshape(n, d)


def _cumsum_lanes(mask, chunk):
    e, n = mask.shape
    r = lax.broadcasted_iota(jnp.int32, (chunk, chunk), 0)
    c = lax.broadcasted_iota(jnp.int32, (chunk, chunk), 1)
    tri = (r < c).astype(BF16)
    carry = jnp.zeros((e, 1), F32)
    out = []
    for j in range(n // chunk):
        m = mask[:, j * chunk:(j + 1) * chunk]
        out.append(jnp.dot(m.astype(BF16), tri, preferred_element_type=F32) + carry)
        carry = carry + jnp.sum(m, axis=1, keepdims=True)
    return jnp.concatenate(out, axis=1)


def _route_select_kernel(lg_ref, aff_ref, pos_ref, *, cap, chunk):
    lg = lg_ref[...]
    mx = jnp.max(lg, axis=0, keepdims=True)
    ex = jnp.exp(lg - mx)
    aff = ex / jnp.sum(ex, axis=0, keepdims=True)
    aff_ref[...] = aff
    e = lg.shape[0]

    def count_above(thr):
        return jnp.sum((aff > thr).astype(F32), axis=1, keepdims=True)

    def exp_step(i, carry):
        a, b, lo, hi = carry
        m = 0.5 * (a + b)
        thr = jnp.exp2(-m)
        ge = count_above(thr) >= cap
        return (jnp.where(ge, a, m), jnp.where(ge, m, b), jnp.where(ge, thr, lo), jnp.where(ge, hi, thr))

    def lin_step(i, carry):
        lo, hi = carry
        mid = 0.5 * (lo + hi)
        ge = count_above(mid) >= cap
        return jnp.where(ge, mid, lo), jnp.where(ge, hi, mid)

    col = lambda v: jnp.full((e, 1), v, F32)
    _, _, lo, hi = lax.fori_loop(0, EXP_BISECT_STEPS, exp_step, (col(-1.0), col(151.0), col(-1.0), col(2.0)))
    lo, hi = lax.fori_loop(0, VALUE_BISECT_STEPS, lin_step, (lo, hi))
    gt = aff > hi
    eq = jnp.logical_and(aff > lo, jnp.logical_not(gt))
    need = cap - jnp.sum(gt.astype(F32), axis=1, keepdims=True)
    tie_rank = _cumsum_lanes(eq.astype(F32), chunk)
    sel = jnp.logical_or(gt, jnp.logical_and(eq, tie_rank < need))
    pos = _cumsum_lanes(sel.astype(F32), chunk)
    pos_ref[...] = jnp.where(sel, pos, -1.0)


def _route_select_call(lgt, cap):
    nb, e, n = lgt.shape
    chunk = min(n, 256)
    return pl.pallas_call(
        functools.partial(_route_select_kernel, cap=cap, chunk=chunk), name="route_select",
        grid=(nb,),
        in_specs=[pl.BlockSpec((None, e, n), lambda b: (b, 0, 0))],
        out_specs=[pl.BlockSpec((None, e, n), lambda b: (b, 0, 0)),
                   pl.BlockSpec((None, e, n), lambda b: (b, 0, 0))],
        out_shape=[jax.ShapeDtypeStruct((nb, e, n), F32),
                   jax.ShapeDtypeStruct((nb, e, n), F32)],
        compiler_params=_params("arbitrary"),
    )(lgt)


def _route_compact_kernel(pos_ref, affc_ref, o_ref, vals_ref, *, cap, chunk, slot_tile):
    e = pl.program_id(1)
    n = affc_ref.shape[0]
    lane = lax.broadcasted_iota(jnp.int32, affc_ref.shape, 1)
    gate = jnp.sum(jnp.where(lane == e, affc_ref[...], 0.0), axis=1, keepdims=True)
    g1, g2, g3 = _split3(gate)
    tok = lax.broadcasted_iota(jnp.int32, (n, LANES), 0)
    col = lax.broadcasted_iota(jnp.int32, (n, LANES), 1)
    vals = jnp.where(col == 0, (tok // LANES).astype(F32),
                     jnp.where(col == 1, (tok % LANES).astype(F32),
                               jnp.where(col == 2, g1.astype(F32),
                                         jnp.where(col == 3, g2.astype(F32),
                                                   jnp.where(col == 4, g3.astype(F32), 0.0)))))
    vals_ref[...] = vals.astype(BF16)
    o_ref[...] = jnp.zeros(o_ref.shape, F32)
    slot = lax.broadcasted_iota(jnp.int32, (slot_tile, chunk), 0).astype(F32)

    n_tiles = cap // slot_tile
    assert chunk <= slot_tile or n_tiles <= 2

    def body(c, carry):
        start = pl.multiple_of(c * chunk, chunk)
        prow = pos_ref[:, pl.ds(start, chunk)]
        v = vals_ref[pl.ds(start, chunk), :]
        first = jnp.min(jnp.where(prow >= 0.0, prow, float(cap))).astype(jnp.int32)
        j0 = first // slot_tile
        for dj in range(min(2, n_tiles)):
            j = j0 + dj

            @pl.when(j < n_tiles)
            def _():
                off = pl.multiple_of(j * slot_tile, slot_tile)
                onehot = (prow == slot + off.astype(F32)).astype(BF16)
                o_ref[pl.ds(off, slot_tile), :] += jnp.dot(onehot, v, preferred_element_type=F32)
        return carry

    lax.fori_loop(0, n // chunk, body, 0)


def _route_compact_call(pos, affc, cap):
    nb, e, _, n = pos.shape
    chunk = min(n, 256)
    slot_tile = min(cap, 256)
    return pl.pallas_call(
        functools.partial(_route_compact_kernel, cap=cap, chunk=chunk, slot_tile=slot_tile), name="route_compact",
        grid=(nb, e),
        in_specs=[pl.BlockSpec((None, None, 1, n), lambda b, x: (b, x, 0, 0)),
                  pl.BlockSpec((None, n, e), lambda b, x: (b, 0, 0))],
        out_specs=pl.BlockSpec((None, None, cap, LANES), lambda b, x: (b, x, 0, 0)),
        out_shape=jax.ShapeDtypeStruct((nb, e, cap, LANES), F32),
        scratch_shapes=[pltpu.VMEM((n, LANES), BF16)],
        compiler_params=_params("arbitrary", "arbitrary"),
    )(pos, affc)


def _route(logits_seg, cap, row_base):
    nb, n, e = logits_seg.shape
    aff, pos = _route_select_call(logits_seg.transpose(0, 2, 1), cap)
    packed = _route_compact_call(pos[:, :, None, :], aff.transpose(0, 2, 1), cap)
    tok = (packed[..., 0] * LANES + packed[..., 1]).astype(jnp.int32)
    gate = (packed[..., 2] + packed[..., 3]) + packed[..., 4]
    rows = tok + row_base[:, None, None]
    return (rows.transpose(1, 0, 2).reshape(e, nb * cap),
            gate.transpose(1, 0, 2).reshape(e, nb * cap))


def _row_copy(src, src_row, dst, dst_row, sem):
    return pltpu.make_async_copy(src.at[pl.ds(src_row, 1)], dst.at[pl.ds(dst_row, 1)], sem)


def _expert_kernel(idx_ref, idxn_ref, gate_ref, h_hbm, wg_ref, wu_ref, wd_ref, acc_hbm, out_hbm,
                   xg_ref, r_ref, xb_ref, y_ref, sem, *, m, d, fc, step_rows, unroll):
    del acc_hbm
    e = pl.program_id(0)
    f = pl.program_id(1)
    ne = pl.num_programs(0)
    nf = pl.num_programs(1)

    def for_rows(fn):
        def body(i, carry):
            for u in range(unroll):
                fn(i * unroll + u)
            return carry
        lax.fori_loop(0, m // unroll, body, 0)

    def wait_all(buf, s):
        pltpu.make_async_copy(buf, buf, sem.at[s]).wait()

    @pl.when(jnp.logical_and(e == 0, f == 0))
    def _():
        for_rows(lambda j: _row_copy(h_hbm, idx_ref[0, j], xg_ref, j, sem.at[0]).start())

    @pl.when(f == 0)
    def _():
        wait_all(xg_ref, 0)
        xb_ref[...] = xg_ref[...].astype(BF16)
        y_ref[...] = jnp.zeros(y_ref.shape, F32)

    for u in range(step_rows):
        j = f * step_rows + u
        _row_copy(h_hbm, idxn_ref[0, j], xg_ref, j, sem.at[0]).start()
        _row_copy(out_hbm, idx_ref[0, j], r_ref, j, sem.at[1]).start()

    xb = xb_ref[...]
    gate_h = jnp.dot(xb, wg_ref[...].astype(BF16), preferred_element_type=F32)
    up_h = jnp.dot(xb, wu_ref[...].astype(BF16), preferred_element_type=F32)
    hid = gate_h * jax.nn.sigmoid(gate_h) * up_h
    g = gate_ref[...]
    hid = jnp.concatenate([hid[:, c * LANES:(c + 1) * LANES] * g for c in range(fc // LANES)], axis=1)
    y_ref[...] += jnp.dot(hid.astype(BF16), wd_ref[...].astype(BF16), preferred_element_type=F32)

    @pl.when(f == nf - 1)
    def _():
        wait_all(r_ref, 1)
        r_ref[...] += y_ref[...]
        for_rows(lambda j: _row_copy(r_ref, j, out_hbm, idx_ref[0, j], sem.at[2]).start())
        wait_all(r_ref, 2)

    @pl.when(jnp.logical_and(e == ne - 1, f == nf - 1))
    def _():
        wait_all(xg_ref, 0)


def _expert_call(idx, gates_b, h2, w_gate, w_up, w_down, acc0, *, layer):
    n_exp, m = idx.shape
    n, d = h2.shape
    ff = w_gate.shape[3]
    fc = 256 if ff % 256 == 0 else ff
    nf = ff // fc
    unroll = 8
    assert m % nf == 0 and m % unroll == 0 and fc % LANES == 0
    nxt = lambda e, f: (jnp.minimum(e + 1, n_exp - 1), 0, 0)
    grid_spec = pl.GridSpec(
        grid=(n_exp, nf),
        in_specs=[pl.BlockSpec((None, 1, m), lambda e, f: (e, 0, 0), memory_space=pltpu.SMEM),
                  pl.BlockSpec((None, 1, m), nxt, memory_space=pltpu.SMEM),
                  pl.BlockSpec((None, m, LANES), lambda e, f: (e, 0, 0)),
                  pl.BlockSpec(memory_space=pl.ANY),
                  pl.BlockSpec((None, None, d, fc), lambda e, f: (layer, e, 0, f)),
                  pl.BlockSpec((None, None, d, fc), lambda e, f: (layer, e, 0, f)),
                  pl.BlockSpec((None, None, fc, d), lambda e, f: (layer, e, f, 0)),
                  pl.BlockSpec(memory_space=pl.ANY)],
        out_specs=pl.BlockSpec(memory_space=pl.ANY),
        scratch_shapes=[pltpu.VMEM((m, d), F32),
                        pltpu.VMEM((m, d), F32),
                        pltpu.VMEM((m, d), BF16),
                        pltpu.VMEM((m, d), F32),
                        pltpu.SemaphoreType.DMA((3,))],
    )
    idx3 = idx[:, None, :]
    return pl.pallas_call(
        functools.partial(_expert_kernel, m=m, d=d, fc=fc, step_rows=m // nf, unroll=unroll), name="moe_experts",
        grid_spec=grid_spec,
        out_shape=jax.ShapeDtypeStruct((n, d), F32),
        input_output_aliases={7: 0},
        compiler_params=_params("arbitrary", "arbitrary"),
    )(idx3, idx3, gates_b, h2, w_gate, w_up, w_down, acc0)


def _moe_norm_kernel(x_ref, y_ref, mod_ref, g_ref, beta_ref, o_ref, *, d, alpha):
    o_ref[...] = _layer_norm(alpha * x_ref[...] + mod_ref[:, 5 * d:6 * d] * y_ref[...],
                             g_ref[...], beta_ref[...])


def _moe_norm_call(x1, y, modl, ln_g, ln_b, *, nb, s_len, n_ctx_tiles, alpha):
    n, d = x1.shape
    nt = s_len // ROW_TILE
    row = lambda b, i: (b * nt + i, 0)
    const = lambda b, i: (0, 0)
    return pl.pallas_call(
        functools.partial(_moe_norm_kernel, d=d, alpha=alpha), name="moe_norm",
        grid=(nb, nt),
        in_specs=[pl.BlockSpec((ROW_TILE, d), row),
                  pl.BlockSpec((ROW_TILE, d), row),
                  _mod_spec(d, n_ctx_tiles, nb),
                  pl.BlockSpec((1, d), const),
                  pl.BlockSpec((1, d), const)],
        out_specs=pl.BlockSpec((ROW_TILE, d), row),
        out_shape=jax.ShapeDtypeStruct((n, d), F32),
        compiler_params=_params("arbitrary", "arbitrary"),
    )(x1, y, modl, ln_g, ln_b)


def _rope_tables(seq, ctx_len):
    axis_dim = HEAD_DIM // 2
    rows = seq // GRID_W
    row = jnp.repeat(jnp.arange(rows, dtype=F32), GRID_W)
    col = jnp.tile(jnp.arange(GRID_W, dtype=F32), rows)
    inv_freq = ROPE_THETA ** (-jnp.arange(0, axis_dim, 2, dtype=F32) / axis_dim)
    ang = jnp.concatenate([row[:, None] * inv_freq, col[:, None] * inv_freq], axis=-1)
    cos = jnp.repeat(jnp.cos(ang), 2, axis=1)
    sin = jnp.repeat(jnp.sin(ang), 2, axis=1) * jnp.tile(jnp.array([-1.0, 1.0], F32), HEAD_DIM // 2)
    cos = jnp.concatenate([jnp.ones((ctx_len, HEAD_DIM), F32), cos], axis=0)
    sin = jnp.concatenate([jnp.zeros((ctx_len, HEAD_DIM), F32), sin], axis=0)
    rep = LANES // HEAD_DIM
    return jnp.tile(cos, (1, rep)), jnp.tile(sin, (1, rep))


def _split3_host(w):
    w1 = w.astype(BF16)
    r1 = w - w1.astype(F32)
    w2 = r1.astype(BF16)
    w3 = (r1 - w2.astype(F32)).astype(BF16)
    return jnp.concatenate([w1, w2, w3], axis=1)


def kernel(x, c, ctx, c_ctx, ada_w, ada_b, ln_g, ln_b, attn_in_w, attn_in_b, q_norm_g, k_norm_g, conv_w, conv_b, conv_ln_g, conv_ln_b, attn_out_w, attn_out_b, fourier_out_w, fourier_out_b, router_w, expert_w_gate, expert_w_up, expert_w_down):
    nb, seq, d = x.shape
    ctx_len = ctx.shape[1]
    depth = ada_w.shape[0]
    n_exp = router_w.shape[2]
    conv = conv_w.shape[2]
    attn = N_Q_HEADS * HEAD_DIM
    kvd = N_KV_HEADS * HEAD_DIM
    s_len = ctx_len + seq
    n = nb * s_len
    assert ctx_len % ROW_TILE == 0 and seq % ROW_TILE == 0 and seq % DFT_INNER == 0
    assert nb + 1 <= SUBLANES and LANES % HEAD_DIM == 0 and attn % LANES == 0 and kvd % LANES == 0
    n_ctx_tiles = ctx_len // ROW_TILE
    alpha = (2 * depth) ** 0.25
    cap_lat = CAPACITY_FACTOR * seq // n_exp
    cap_ctx = CAPACITY_FACTOR * ctx_len // n_exp
    tile_kw = dict(nb=nb, s_len=s_len, n_ctx_tiles=n_ctx_tiles)

    mods = _mods_call(jnp.concatenate([c, c_ctx[None, :]], axis=0), ada_w, ada_b)
    cos_t, sin_t = _rope_tables(seq, ctx_len)
    consts = _dft_constants(d, seq, ctx_len)
    rep = LANES // HEAD_DIM
    base = jnp.arange(nb, dtype=jnp.int32) * s_len

    xs = jnp.concatenate([ctx, x], axis=1).reshape(n, d)
    for i in range(depth):
        j = i // 2
        modl = mods[i][:, None, :]
        if i % 2 == 0:
            q, k, v, u = _inproj_call(
                xs, modl, attn_in_w[j].astype(BF16), attn_in_b[j][None, :],
                jnp.tile(q_norm_g[j], rep)[None, :], jnp.tile(k_norm_g[j], rep)[None, :],
                cos_t, sin_t, conv=conv, **tile_kw)
            qh = q.reshape(nb, s_len, N_Q_HEADS, HEAD_DIM).transpose(0, 2, 1, 3)
            kh = k.reshape(nb, s_len, N_KV_HEADS, HEAD_DIM).transpose(0, 2, 1, 3)
            vth = v.reshape(nb, s_len, N_KV_HEADS, HEAD_DIM).transpose(0, 2, 3, 1)
            score_bound = (1.02 * HEAD_DIM ** 0.5 * LOG2_E) * jnp.max(jnp.abs(q_norm_g[j])) * jnp.max(jnp.abs(k_norm_g[j]))
            q_lat, q_ctx = qh[:, :, ctx_len:], qh[:, :, :ctx_len]

            def attend(bounded):
                return lambda: (_attn_call(q_lat, kh, vth, n_k=s_len, bounded=bounded),
                                _attn_call(q_ctx, kh, vth, n_k=ctx_len, bounded=bounded))

            o_lat, o_ctx = lax.cond(score_bound < MAX_UNSHIFTED_LOG2_SCORE, attend(True), attend(False))
            att = jnp.concatenate([o_ctx, o_lat], axis=3).transpose(0, 3, 1, 2).reshape(n, attn)
            cv = _conv_call(u, conv_w[j], conv_b[j][None, :], conv_ln_g[j][None, :], conv_ln_b[j][None, :],
                            **tile_kw)
            lhs = jnp.concatenate([att, cv], axis=1)
            w_o, b_o = attn_out_w[j], attn_out_b[j]
        else:
            lhs = _fourier_mix(xs, modl, consts, nb=nb, s_len=s_len, ctx_len=ctx_len, n_ctx_tiles=n_ctx_tiles)
            w_o, b_o = fourier_out_w[j], fourier_out_b[j]
        x1, h2, logits = _mixout_call(lhs, w_o.astype(BF16), b_o[None, :], xs, modl,
                                      ln_g[i, 0][None, :], ln_b[i, 0][None, :], _split3_host(router_w[i]),
                                      alpha=alpha, **tile_kw)
        lg = logits.reshape(nb, s_len, n_exp)
        rows_l, gates_l = _route(lg[:, ctx_len:], cap_lat, base + ctx_len)
        rows_c, gates_c = _route(lg[:, :ctx_len], cap_ctx, base)
        idx = jnp.concatenate([rows_l, rows_c], axis=1)
        gates = jnp.concatenate([gates_l, gates_c], axis=1)
        gates_b = jnp.broadcast_to(gates[:, :, None], gates.shape + (LANES,))
        y = _expert_call(idx, gates_b, h2, expert_w_gate, expert_w_up, expert_w_down,
                         jnp.zeros((n, d), F32), layer=i)
        xs = _moe_norm_call(x1, y, modl, ln_g[i, 1][None, :], ln_b[i, 1][None, :], alpha=alpha, **tile_kw)
    return xs.reshape(nb, s_len, d)[:, ctx_len:]
```
